```python
import math
import jax, jax.numpy as jnp
from jax import lax
import numpy as np


D_MODEL = 1024
BATCH = 2
SEQ = 8192
DEPTH = 2
DEC_BATCH = 32
DEC_SEQ = 1
PAST_LEN = 16384
PAGE_SIZE = 128

N_HEADS = 8
N_KV_HEADS = 4
GROUP = N_HEADS // N_KV_HEADS
QK_DIM = D_MODEL // N_HEADS // 2
V_DIM = 2 * QK_DIM
ROT_DIM = QK_DIM // 4
ROPE_THETA = 500000.0
Q_BLOCK = 128
D_RNN = D_MODEL
N_RNN_BLOCKS = 16
RNN_BLOCK = D_RNN // N_RNN_BLOCKS
CONV_W = 4
LRU_C = 8.0
D_FF = ((8 * D_MODEL + 3 * 256 - 1) // (3 * 256)) * 256
ALPHA = (2 * DEPTH) ** 0.25
BETA = (8 * DEPTH) ** -0.25
EPS = 1e-5
_QC = N_HEADS * 2 * QK_DIM
_KC = N_KV_HEADS * 2 * QK_DIM
_VC = N_KV_HEADS * V_DIM
SPLITS = (_QC, _QC + _KC, _QC + _KC + _VC, _QC + _KC + _VC + D_RNN,
          _QC + _KC + _VC + 2 * D_RNN, _QC + _KC + _VC + 2 * D_RNN + D_MODEL)
N_IN = _QC + _KC + _VC + 2 * D_RNN + 2 * D_MODEL

kernel_name = 'hybrid_diffattn_rglru_step'


def layer_norm(x, g, b):
    xf = x.astype(jnp.float32)
    mu = jnp.mean(xf, -1, keepdims=True)
    xc = xf - mu
    var = jnp.mean(xc * xc, -1, keepdims=True)
    return (xc * lax.rsqrt(var + EPS) * g + b).astype(x.dtype)


def rope_partial(x, pos):
    half = ROT_DIM // 2
    inv = ROPE_THETA ** (-jnp.arange(half, dtype=jnp.float32) * 2.0 / ROT_DIM)
    ang = pos.astype(jnp.float32)[:, None] * inv[None, :]
    shape = (1, pos.shape[0]) + (1,) * (x.ndim - 3) + (half,)
    cos = jnp.cos(ang).reshape(shape).astype(x.dtype)
    sin = jnp.sin(ang).reshape(shape).astype(x.dtype)
    x1, x2, rest = x[..., :half], x[..., half:ROT_DIM], x[..., ROT_DIM:]
    return jnp.concatenate([x1 * cos - x2 * sin, x2 * cos + x1 * sin, rest], axis=-1)


def diff_attend(q, k, v, q_pos, k_pos, lam):
    s = jnp.einsum('bqhgcd,bkhcd->bchgqk', q, k, preferred_element_type=jnp.float32) * (QK_DIM ** -0.5)
    mask = q_pos[:, None] >= k_pos[None, :]
    p = jax.nn.softmax(jnp.where(mask, s, -jnp.inf), axis=-1)
    att = p[:, 0] - lam * p[:, 1]
    return jnp.einsum('bhgqk,bkhd->bqhgd', att.astype(v.dtype), v)


def diff_attend_blocked(q, k, v, pos, lam):
    b, s = q.shape[:2]
    nb = s // Q_BLOCK
    qb = jnp.swapaxes(q.reshape((b, nb, Q_BLOCK) + q.shape[2:]), 0, 1)
    pb = pos.reshape(nb, Q_BLOCK)
    ob = lax.map(lambda a: diff_attend(a[0], k, v, a[1], pos, lam), (qb, pb))
    return jnp.swapaxes(ob, 0, 1).reshape((b, s) + ob.shape[3:])


def lru_scan(a, u, h0):
    def step(h, au):
        h = au[0] * h + au[1]
        return h, h
    h_last, hs = lax.scan(step, h0, (jnp.swapaxes(a, 0, 1), jnp.swapaxes(u, 0, 1)))
    return jnp.swapaxes(hs, 0, 1), h_last


def mixer(h, pos, lam_init, mw, k_past, v_past, h0, conv_buf):
    (w_in, lq1, lk1, lq2, lk2, subln_g, conv_w, conv_b, w_rg, b_rg, w_ig, b_ig,
     lru_lambda, w_pa, w_pr, w_o) = mw
    b, s, _ = h.shape
    q, k, v, xr, yr, ga, gr = jnp.split(h @ w_in, SPLITS, axis=-1)
    q = rope_partial(q.reshape(b, s, N_KV_HEADS, GROUP, 2, QK_DIM), pos)
    k = rope_partial(k.reshape(b, s, N_KV_HEADS, 2, QK_DIM), pos)
    v = v.reshape(b, s, N_KV_HEADS, V_DIM)
    f32 = jnp.float32
    lam = (jnp.exp(jnp.dot(lq1.astype(f32), lk1.astype(f32)))
           - jnp.exp(jnp.dot(lq2.astype(f32), lk2.astype(f32))) + lam_init)
    if k_past is None:
        o = diff_attend_blocked(q, k, v, pos, lam)
    else:
        k_all = jnp.concatenate([k_past, k], axis=1)
        v_all = jnp.concatenate([v_past, v], axis=1)
        o = diff_attend(q, k_all, v_all, pos, jnp.arange(k_all.shape[1]), lam)
    of = o.astype(f32)
    of = of * lax.rsqrt(jnp.mean(of * of, -1, keepdims=True) + EPS) * subln_g * (1.0 - lam_init)
    p_a = of.astype(h.dtype).reshape(b, s, N_HEADS * V_DIM) @ w_pa
    xpad = jnp.concatenate([conv_buf.astype(xr.dtype), xr], axis=1)
    xc = conv_b + sum(xpad[:, j:j + s] * conv_w[j] for j in range(CONV_W))
    new_conv = xpad[:, -(CONV_W - 1):]
    xb = xc.reshape(b, s, N_RNN_BLOCKS, RNN_BLOCK)
    r = jax.nn.sigmoid(jnp.einsum('bsnd,nde->bsne', xb, w_rg).reshape(b, s, D_RNN) + b_rg)
    i = jax.nn.sigmoid(jnp.einsum('bsnd,nde->bsne', xb, w_ig).reshape(b, s, D_RNN) + b_ig)
    log_a = -LRU_C * r.astype(f32) * jax.nn.softplus(-lru_lambda.astype(f32))
    a = jnp.exp(log_a)
    u = jnp.sqrt(-jnp.expm1(2.0 * log_a)) * (i.astype(f32) * xc.astype(f32))
    hs, h_last = lru_scan(a, u, h0.astype(f32))
    p_r = (jax.nn.gelu(yr) * hs.astype(h.dtype)) @ w_pr
    out = (jax.nn.sigmoid(ga) * p_a + jax.nn.sigmoid(gr) * p_r) @ w_o
    return out, k, v, h_last.astype(h.dtype), new_conv


def layer(x, c, pos, lam_init, mw, bw, k_past, v_past, h0, conv_buf):
    w_ada, b_ada, ln1_g, ln1_b, ln2_g, ln2_b, w_ffn_in, w_ffn_out = bw
    mod = (jax.nn.silu(c) @ w_ada + b_ada)[:, None, :]
    sh_m, sc_m, g_m, sh_f, sc_f, g_f = jnp.split(mod, 6, axis=-1)
    mix, k, v, h_last, new_conv = mixer(x * (1 + sc_m) + sh_m, pos, lam_init, mw,
                                        k_past, v_past, h0, conv_buf)
    x = layer_norm(ALPHA * x + g_m * mix, ln1_g, ln1_b)
    gate, up = jnp.split((x * (1 + sc_f) + sh_f) @ w_ffn_in, 2, axis=-1)
    x = layer_norm(ALPHA * x + g_f * ((jax.nn.silu(gate) * up) @ w_ffn_out), ln2_g, ln2_b)
    return x, k, v, h_last, new_conv


def setup_inputs(seed: int = 0) -> dict:
    key = jax.random.key(seed)
    ks = iter(jax.random.split(key, 40))
    f32 = jnp.float32
    def nrm(shape, scale):
        return jax.random.normal(next(ks), shape, f32) * scale
    n_pages = PAST_LEN // PAGE_SIZE
    n_used = DEC_BATCH * n_pages
    n_pool = n_used + max(1, n_used // 4)
    d = {}
    d['x_prompt'] = nrm((BATCH, SEQ, D_MODEL), 1.0)
    d['x_sample'] = nrm((DEC_BATCH, DEC_SEQ, D_MODEL), 1.0)
    d['c_prompt'] = nrm((BATCH, D_MODEL), 1.0)
    d['c_sample'] = nrm((DEC_BATCH, D_MODEL), 1.0)
    d['cache_k'] = nrm((DEPTH, n_pool, PAGE_SIZE, N_KV_HEADS, 2, QK_DIM), 1.0)
    d['cache_v'] = nrm((DEPTH, n_pool, PAGE_SIZE, N_KV_HEADS, V_DIM), 1.0)
    d['page_table'] = jax.random.permutation(next(ks), n_pool)[:n_used].reshape(DEC_BATCH, n_pages).astype(jnp.int32)
    d['state_h'] = nrm((DEPTH, DEC_BATCH, D_RNN), 0.5)
    d['state_conv'] = nrm((DEPTH, DEC_BATCH, CONV_W - 1, D_RNN), 1.0)
    d['w_ada'] = nrm((DEPTH, D_MODEL, 6 * D_MODEL), D_MODEL ** -0.5)
    d['b_ada'] = nrm((DEPTH, 6 * D_MODEL), 0.02)
    d['w_in'] = nrm((DEPTH, D_MODEL, N_IN), D_MODEL ** -0.5)
    d['lambda_q1'] = nrm((DEPTH, QK_DIM), 0.1)
    d['lambda_k1'] = nrm((DEPTH, QK_DIM), 0.1)
    d['lambda_q2'] = nrm((DEPTH, QK_DIM), 0.1)
    d['lambda_k2'] = nrm((DEPTH, QK_DIM), 0.1)
    d['subln_g'] = 1.0 + nrm((DEPTH, V_DIM), 0.02)
    d['conv_w'] = nrm((DEPTH, CONV_W, D_RNN), CONV_W ** -0.5)
    d['conv_b'] = nrm((DEPTH, D_RNN), 0.02)
    d['w_rgate'] = nrm((DEPTH, N_RNN_BLOCKS, RNN_BLOCK, RNN_BLOCK), RNN_BLOCK ** -0.5)
    d['b_rgate'] = nrm((DEPTH, D_RNN), 0.02)
    d['w_igate'] = nrm((DEPTH, N_RNN_BLOCKS, RNN_BLOCK, RNN_BLOCK), RNN_BLOCK ** -0.5)
    d['b_igate'] = nrm((DEPTH, D_RNN), 0.02)
    a_c = jax.random.uniform(next(ks), (DEPTH, D_RNN), f32, 0.9, 0.999)
    a0 = a_c ** (1.0 / LRU_C)
    d['lru_lambda'] = jnp.log(a0) - jnp.log1p(-a0)
    d['w_pa'] = nrm((DEPTH, N_HEADS * V_DIM, D_MODEL), (N_HEADS * V_DIM) ** -0.5)
    d['w_pr'] = nrm((DEPTH, D_RNN, D_MODEL), D_RNN ** -0.5)
    d['w_o'] = nrm((DEPTH, D_MODEL, D_MODEL), BETA * D_MODEL ** -0.5)
    d['ln1_g'] = 1.0 + nrm((DEPTH, D_MODEL), 0.02)
    d['ln1_b'] = nrm((DEPTH, D_MODEL), 0.02)
    d['ln2_g'] = 1.0 + nrm((DEPTH, D_MODEL), 0.02)
    d['ln2_b'] = nrm((DEPTH, D_MODEL), 0.02)
    d['w_ffn_in'] = nrm((DEPTH, D_MODEL, 2 * D_FF), D_MODEL ** -0.5)
    d['w_ffn_out'] = nrm((DEPTH, D_FF, D_MODEL), BETA * D_FF ** -0.5)
    return d


def reference(x_prompt, x_sample, c_prompt, c_sample, cache_k, cache_v, page_table, state_h, state_conv,
              w_ada, b_ada, w_in, lambda_q1, lambda_k1, lambda_q2, lambda_k2, subln_g, conv_w, conv_b,
              w_rgate, b_rgate, w_igate, b_igate, lru_lambda, w_pa, w_pr, w_o,
              ln1_g, ln1_b, ln2_g, ln2_b, w_ffn_in, w_ffn_out):
    bp, sp = x_prompt.shape[:2]
    bs, ss = x_sample.shape[:2]
    past_len = page_table.shape[1] * PAGE_SIZE
    pos_p = jnp.arange(sp)
    pos_s = past_len + jnp.arange(ss)
    yp, ys = x_prompt, x_sample
    kp_l, vp_l, hp_l, cp_l, ks_l, vs_l, hs_l, cs_l = [], [], [], [], [], [], [], []
    for l in range(DEPTH):
        lam_init = 0.8 - 0.6 * math.exp(-0.3 * l)
        mw = (w_in[l], lambda_q1[l], lambda_k1[l], lambda_q2[l], lambda_k2[l], subln_g[l], conv_w[l], conv_b[l],
              w_rgate[l], b_rgate[l], w_igate[l], b_igate[l], lru_lambda[l], w_pa[l], w_pr[l], w_o[l])
        bw = (w_ada[l], b_ada[l], ln1_g[l], ln1_b[l], ln2_g[l], ln2_b[l], w_ffn_in[l], w_ffn_out[l])
        h0 = jnp.zeros((bp, D_RNN), jnp.float32)
        buf0 = jnp.zeros((bp, CONV_W - 1, D_RNN), yp.dtype)
        yp, k_new, v_new, h_new, c_new = layer(yp, c_prompt, pos_p, lam_init, mw, bw, None, None, h0, buf0)
        kp_l.append(k_new); vp_l.append(v_new); hp_l.append(h_new); cp_l.append(c_new)
        k_past = cache_k[l, page_table].reshape(bs, past_len, N_KV_HEADS, 2, QK_DIM)
        v_past = cache_v[l, page_table].reshape(bs, past_len, N_KV_HEADS, V_DIM)
        ys, k_new, v_new, h_new, c_new = layer(ys, c_sample, pos_s, lam_init, mw, bw,
                                               k_past, v_past, state_h[l], state_conv[l])
        ks_l.append(k_new); vs_l.append(v_new); hs_l.append(h_new); cs_l.append(c_new)
    return (yp, ys, jnp.stack(kp_l), jnp.stack(vp_l), jnp.stack(hp_l), jnp.stack(cp_l),
            jnp.stack(ks_l), jnp.stack(vs_l), jnp.stack(hs_l), jnp.stack(cs_l))
```

```python
import functools
import math

import jax
import jax.numpy as jnp
from jax import lax
from jax.experimental import pallas as pl
from jax.experimental.pallas import tpu as pltpu

D_MODEL = 1024
DEPTH = 2
PAGE_SIZE = 128
N_HEADS = 8
N_KV_HEADS = 4
GROUP = N_HEADS // N_KV_HEADS
QK_DIM = D_MODEL // N_HEADS // 2
V_DIM = 2 * QK_DIM
ROT_DIM = QK_DIM // 4
ROPE_THETA = 500000.0
D_RNN = D_MODEL
N_RNN_BLOCKS = 16
RNN_BLOCK = D_RNN // N_RNN_BLOCKS
CONV_W = 4
LRU_C = 8.0
D_FF = ((8 * D_MODEL + 3 * 256 - 1) // (3 * 256)) * 256
ALPHA = (2 * DEPTH) ** 0.25
EPS = 1e-5
QC = N_HEADS * 2 * QK_DIM
KC = N_KV_HEADS * 2 * QK_DIM
VC = N_KV_HEADS * V_DIM
N_IN = QC + KC + VC + 2 * D_RNN + 2 * D_MODEL
QK_SCALE = QK_DIM ** -0.5

LANES = 128
SUBLANES = 8
VMEM_LIMIT = 56 * 1024 * 1024

F32 = jnp.float32
BF16 = jnp.bfloat16


def _params(sem, vmem=VMEM_LIMIT):
    return pltpu.CompilerParams(dimension_semantics=sem, vmem_limit_bytes=vmem)


def _resident(shape):
    nd = len(shape)
    return pl.BlockSpec(shape, lambda *_: (0,) * nd, pipeline_mode=pl.Buffered(1))


def _layer_norm(y, g, b):
    mu = jnp.mean(y, axis=-1, keepdims=True)
    yc = y - mu
    var = jnp.mean(yc * yc, axis=-1, keepdims=True)
    return yc * lax.rsqrt(var + EPS) * g + b


def _lam_value(lq1, lk1, lq2, lk2, lam_init):
    d1 = jnp.sum(lq1 * lk1, axis=-1, keepdims=True)
    d2 = jnp.sum(lq2 * lk2, axis=-1, keepdims=True)
    return jnp.exp(d1) - jnp.exp(d2) + lam_init


def _rope_table_kernel(inv_ref, cos_ref, sin_ref, *, rows, offset):
    i = pl.program_id(0)
    pos = (offset + i * rows + lax.broadcasted_iota(jnp.int32, (rows, LANES), 0)).astype(F32)
    ang = pos * inv_ref[...]
    d = lax.broadcasted_iota(jnp.int32, (rows, LANES), 1) % QK_DIM
    half = ROT_DIM // 2
    c = jnp.cos(ang)
    s = jnp.sin(ang)
    cos_ref[...] = jnp.where(d < ROT_DIM, c, 1.0)
    sin_ref[...] = jnp.where(d < half, -s, jnp.where(d < ROT_DIM, s, 0.0))


def _rope_tables(inv_lane, n_pos, offset):
    rows = min(1024, max(SUBLANES, n_pos))
    n_rows = max(SUBLANES, n_pos)
    out = jax.ShapeDtypeStruct((n_rows, LANES), F32)
    cos_t, sin_t = pl.pallas_call(
        functools.partial(_rope_table_kernel, rows=rows, offset=offset),
        grid=(n_rows // rows,),
        in_specs=[pl.BlockSpec((1, LANES), lambda i: (0, 0))],
        out_specs=[pl.BlockSpec((rows, LANES), lambda i: (i, 0))] * 2,
        out_shape=[out, out],
        compiler_params=_params(("parallel",)),
        name="rope_tables",
    )(inv_lane)
    return cos_t[:n_pos], sin_t[:n_pos]


def _rope(x, cos_t, sin_t):
    d = lax.broadcasted_iota(jnp.int32, x.shape, 1) % QK_DIM
    partner = jnp.where(d < ROT_DIM // 2,
                        pltpu.roll(x, LANES - ROT_DIM // 2, 1),
                        pltpu.roll(x, ROT_DIM // 2, 1))
    return x * cos_t + partner * sin_t


def _ada_kernel(c_ref, w_ref, b_ref, o_ref):
    a = jax.nn.silu(c_ref[...]).astype(BF16)
    o_ref[...] = jnp.dot(a, w_ref[...].astype(BF16), preferred_element_type=F32) + b_ref[...]


def _ada_mod(c_all, w_ada, b_ada):
    rows = c_all.shape[0]
    n = w_ada.shape[-1]
    bn = 1536
    return pl.pallas_call(
        _ada_kernel,
        grid=(DEPTH, n // bn),
        in_specs=[pl.BlockSpec((rows, D_MODEL), lambda l, j: (0, 0)),
                  pl.BlockSpec((None, D_MODEL, bn), lambda l, j: (l, 0, j)),
                  pl.BlockSpec((None, 1, bn), lambda l, j: (l, 0, j))],
        out_specs=pl.BlockSpec((None, rows, bn), lambda l, j: (l, 0, j)),
        out_shape=jax.ShapeDtypeStruct((DEPTH, rows, n), F32),
        compiler_params=_params(("parallel", "parallel")),
        name="ada_mod",
    )(c_all, w_ada, b_ada.reshape(DEPTH, 1, n))


def _inproj_kernel(x_ref, sc_ref, sh_ref, cos_ref, sin_ref, w_ref,
                   q_ref, kf_ref, vf_ref, kb_ref, vb_ref, xr_ref, yr_ref, ga_ref, gr_ref):
    h = (x_ref[...] * (1.0 + sc_ref[...]) + sh_ref[...]).astype(BF16)
    cos_t = cos_ref[...]
    sin_t = sin_ref[...]

    def proj(lo, hi):
        return jnp.dot(h, w_ref[:, lo:hi], preferred_element_type=F32)

    q = proj(0, QC)
    for j in range(QC // LANES):
        sl = slice(j * LANES, (j + 1) * LANES)
        q_ref[:, sl] = (_rope(q[:, sl], cos_t, sin_t) * QK_SCALE).astype(BF16)
    k = proj(QC, QC + KC)
    for j in range(KC // LANES):
        sl = slice(j * LANES, (j + 1) * LANES)
        kr = _rope(k[:, sl], cos_t, sin_t)
        kf_ref[:, sl] = kr
        kb_ref[:, sl] = kr.astype(BF16)
    v = proj(QC + KC, QC + KC + VC)
    vf_ref[...] = v
    vb_ref[...] = v.astype(BF16)
    o = QC + KC + VC
    xr_ref[...] = proj(o, o + D_RNN)
    yr_ref[...] = proj(o + D_RNN, o + 2 * D_RNN)
    ga_ref[...] = proj(o + 2 * D_RNN, o + 2 * D_RNN + D_MODEL)
    gr_ref[...] = proj(o + 2 * D_RNN + D_MODEL, N_IN)


def _mod_spec(mod, tm):
    r = mod.shape[1]
    if r == 1:
        return pl.BlockSpec((None, 1, D_MODEL), lambda b, i: (b, 0, 0))
    return pl.BlockSpec((None, tm, D_MODEL), lambda b, i: (b, i, 0))


def _row_spec(tm, width):
    return pl.BlockSpec((None, tm, width), lambda b, i: (b, i, 0))


def _inproj(x, sc, sh, cos_t, sin_t, w, tm):
    bx, sx, _ = x.shape
    if cos_t.shape[0] == 1:
        rope_spec = pl.BlockSpec((1, LANES), lambda b, i: (0, 0))
    else:
        rope_spec = pl.BlockSpec((tm, LANES), lambda b, i: (i, 0))
    widths = (QC, KC, VC, KC, VC, D_RNN, D_RNN, D_MODEL, D_MODEL)
    dtypes = (BF16, F32, F32, BF16, BF16, F32, F32, F32, F32)
    return pl.pallas_call(
        _inproj_kernel,
        grid=(bx, sx // tm),
        in_specs=[_row_spec(tm, D_MODEL), _mod_spec(sc, tm), _mod_spec(sh, tm),
                  rope_spec, rope_spec, _resident((D_MODEL, N_IN))],
        out_specs=[_row_spec(tm, wd) for wd in widths],
        out_shape=[jax.ShapeDtypeStruct((bx, sx, wd), dt) for wd, dt in zip(widths, dtypes)],
        compiler_params=_params(("parallel", "parallel")),
        name="inproj",
    )(x, sc, sh, cos_t, sin_t, w)


def _flash_kernel(lq1_ref, lk1_ref, lq2_ref, lk2_ref, g_ref, q_ref, k_ref, v_ref, o_ref,
                  qt_ref, m_ref, l_ref, acc_ref, *, tq, lam_init):
    qi = pl.program_id(2)
    low = lax.broadcasted_iota(jnp.int32, (tq, LANES), 1) < QK_DIM
    for g in range(GROUP):
        qg = q_ref[:, g * LANES:(g + 1) * LANES]
        zero = jnp.zeros_like(qg)
        qt_ref[(2 * g) * tq:(2 * g + 1) * tq, :] = jnp.where(low, qg, zero)
        qt_ref[(2 * g + 1) * tq:(2 * g + 2) * tq, :] = jnp.where(low, zero, qg)
    m_ref[...] = jnp.full(m_ref.shape, -jnp.inf, F32)
    l_ref[...] = jnp.zeros(l_ref.shape, F32)
    acc_ref[...] = jnp.zeros(acc_ref.shape, F32)

    def step(ki, masked):
        start = pl.multiple_of(ki * tq, tq)
        k = k_ref[pl.ds(start, tq), :]
        v = v_ref[pl.ds(start, tq), :]
        s = lax.dot_general(qt_ref[...], k, (((1,), (1,)), ((), ())), preferred_element_type=F32)
        if masked:
            row = lax.broadcasted_iota(jnp.int32, s.shape, 0) % tq
            col = lax.broadcasted_iota(jnp.int32, s.shape, 1)
            s = jnp.where(row >= col, s, -jnp.inf)
        m_prev = m_ref[...]
        m_new = jnp.maximum(m_prev, jnp.max(s, axis=-1, keepdims=True))
        alpha = jnp.exp(m_prev - m_new)
        p = jnp.exp(s - m_new)
        l_ref[...] = alpha * l_ref[...] + jnp.sum(p, axis=-1, keepdims=True)
        acc_ref[...] = alpha * acc_ref[...] + jnp.dot(p.astype(BF16), v, preferred_element_type=F32)
        m_ref[...] = m_new

    def body(ki, carry):
        step(ki, False)
        return carry

    lax.fori_loop(0, qi, body, 0)
    step(qi, True)

    lam = _lam_value(lq1_ref[...], lk1_ref[...], lq2_ref[...], lk2_ref[...], lam_init)
    for g in range(GROUP):
        r0 = slice((2 * g) * tq, (2 * g + 1) * tq)
        r1 = slice((2 * g + 1) * tq, (2 * g + 2) * tq)
        o = acc_ref[r0, :] / l_ref[r0, :] - lam * (acc_ref[r1, :] / l_ref[r1, :])
        of = o * lax.rsqrt(jnp.mean(o * o, axis=-1, keepdims=True) + EPS) * g_ref[...] * (1.0 - lam_init)
        o_ref[:, g * LANES:(g + 1) * LANES] = of.astype(BF16)


def _flash(q, kb, vb, lvecs, subln_g, lam_init, tq):
    b, s, _ = q.shape
    small = pl.BlockSpec((1, QK_DIM), lambda b_, h, i: (0, 0))
    return pl.pallas_call(
        functools.partial(_flash_kernel, tq=tq, lam_init=lam_init),
        grid=(b, N_KV_HEADS, s // tq),
        in_specs=[small, small, small, small,
                  pl.BlockSpec((1, V_DIM), lambda b_, h, i: (0, 0)),
                  pl.BlockSpec((None, tq, 2 * LANES), lambda b_, h, i: (b_, i, h)),
                  pl.BlockSpec((None, s, LANES), lambda b_, h, i: (b_, 0, h)),
                  pl.BlockSpec((None, s, LANES), lambda b_, h, i: (b_, 0, h))],
        out_specs=pl.BlockSpec((None, tq, 2 * LANES), lambda b_, h, i: (b_, i, h)),
        out_shape=jax.ShapeDtypeStruct((b, s, N_HEADS * V_DIM), BF16),
        scratch_shapes=[pltpu.VMEM((2 * GROUP * tq, LANES), BF16),
                        pltpu.VMEM((2 * GROUP * tq, 1), F32),
                        pltpu.VMEM((2 * GROUP * tq, 1), F32),
                        pltpu.VMEM((2 * GROUP * tq, LANES), F32)],
        compiler_params=_params(("parallel", "parallel", "parallel")),
        name="flash_diff_attn",
    )(*lvecs, subln_g, q, kb, vb)


def _decode_kernel(pt_ref, lq1_ref, lk1_ref, lq2_ref, lk2_ref, g_ref, qt_ref, kn_ref, vn_ref, *rest,
                   n_chunk, lam_init):
    k_refs = rest[:n_chunk]
    v_refs = rest[n_chunk:2 * n_chunk]
    o_ref, m_ref, l_ref, acc_ref = rest[2 * n_chunk:]
    j = pl.program_id(1)
    rows = 2 * N_HEADS
    row_head = (lax.broadcasted_iota(jnp.int32, (rows, LANES), 0) % N_HEADS) // GROUP
    qt = qt_ref[...]

    @pl.when(j == 0)
    def _():
        s_new = jnp.sum(qt.astype(F32) * kn_ref[...].astype(F32), axis=-1, keepdims=True)
        m_ref[...] = jnp.broadcast_to(s_new, (rows, LANES))
        l_ref[...] = jnp.ones((rows, LANES), F32)
        vn = vn_ref[...].astype(F32)
        acc = jnp.zeros((rows, LANES), F32)
        for h in range(N_KV_HEADS):
            acc = jnp.where(row_head == h, vn[:, h * V_DIM:(h + 1) * V_DIM], acc)
        acc_ref[...] = acc

    s = jnp.concatenate(
        [jnp.dot(qt, k_refs[i][...].astype(BF16), preferred_element_type=F32) for i in range(n_chunk)],
        axis=1)
    m_prev = m_ref[...]
    m_new = jnp.maximum(m_prev, jnp.max(s, axis=-1, keepdims=True))
    alpha = jnp.exp(m_prev - m_new)
    p = jnp.exp(s - m_new[:, :1])
    l_ref[...] = alpha * l_ref[...] + jnp.sum(p, axis=-1, keepdims=True)
    pb = p.astype(BF16)
    acc = alpha * acc_ref[...]
    for i in range(n_chunk):
        pi = pb[:, i * PAGE_SIZE:(i + 1) * PAGE_SIZE]
        for h in range(N_KV_HEADS):
            vh = v_refs[i][pl.ds(h, PAGE_SIZE, stride=N_KV_HEADS), :].astype(BF16)
            acc = acc + jnp.where(row_head == h, jnp.dot(pi, vh, preferred_element_type=F32), 0.0)
    acc_ref[...] = acc
    m_ref[...] = m_new

    @pl.when(j == pl.num_programs(1) - 1)
    def _():
        lam = _lam_value(lq1_ref[...], lk1_ref[...], lq2_ref[...], lk2_ref[...], lam_init)
        on = acc_ref[...] / l_ref[...]
        o = on[:N_HEADS, :] - lam * on[N_HEADS:, :]
        of = o * lax.rsqrt(jnp.mean(o * o, axis=-1, keepdims=True) + EPS) * g_ref[...] * (1.0 - lam_init)
        o_ref[...] = of.astype(BF16)


def _decode_attn(page_table, qt, kn, vn, cache_kt, cache_vr, layer, lvecs, subln_g, lam_init, n_chunk):
    bs, n_pages = page_table.shape
    rows = 2 * N_HEADS

    def page_spec(i):
        return pl.BlockSpec((None, None, KC, PAGE_SIZE),
                            lambda b, j, pt: (layer, pt[b, j * n_chunk + i], 0, 0))

    small = pl.BlockSpec((1, QK_DIM), lambda b, j, pt: (0, 0))
    grid_spec = pltpu.PrefetchScalarGridSpec(
        num_scalar_prefetch=1,
        grid=(bs, n_pages // n_chunk),
        in_specs=[small, small, small, small,
                  pl.BlockSpec((1, V_DIM), lambda b, j, pt: (0, 0)),
                  pl.BlockSpec((None, rows, KC), lambda b, j, pt: (b, 0, 0)),
                  pl.BlockSpec((None, 1, KC), lambda b, j, pt: (b, 0, 0)),
                  pl.BlockSpec((None, 1, VC), lambda b, j, pt: (b, 0, 0))]
                 + [page_spec(i) for i in range(n_chunk)] * 2,
        out_specs=pl.BlockSpec((None, N_HEADS, V_DIM), lambda b, j, pt: (b, 0, 0)),
        scratch_shapes=[pltpu.VMEM((rows, LANES), F32)] * 3,
    )
    return pl.pallas_call(
        functools.partial(_decode_kernel, n_chunk=n_chunk, lam_init=lam_init),
        grid_spec=grid_spec,
        out_shape=jax.ShapeDtypeStruct((bs, N_HEADS, V_DIM), BF16),
        compiler_params=_params(("parallel", "arbitrary")),
        name="decode_diff_attn",
    )(page_table, *lvecs, subln_g, qt, kn, vn, *([cache_kt] * n_chunk), *([cache_vr] * n_chunk))


def _lru_gates(xc, wr_ref, br_ref, wi_ref, bi_ref, lam_ref):
    xb = xc.astype(BF16)
    r = jax.nn.sigmoid(jnp.dot(xb, wr_ref[...], preferred_element_type=F32) + br_ref[...])
    i = jax.nn.sigmoid(jnp.dot(xb, wi_ref[...], preferred_element_type=F32) + bi_ref[...])
    z = -lam_ref[...]
    softplus = jnp.maximum(z, 0.0) + jnp.log1p(jnp.exp(-jnp.abs(z)))
    log_a = -LRU_C * r * softplus
    a = jnp.exp(log_a)
    u = jnp.sqrt(-jnp.tanh(log_a) * (a * a + 1.0)) * (i * xc)
    return a, u


def _scan_kernel(xr_ref, yr_ref, cw_ref, cb_ref, wr_ref, br_ref, wi_ref, bi_ref, lam_ref,
                 pr_ref, hl_ref, xbuf_ref, a_ref, u_ref, h_ref, *, ts):
    i = pl.program_id(1)
    pad = SUBLANES

    @pl.when(i == 0)
    def _():
        xbuf_ref[0:pad, :] = jnp.zeros((pad, D_RNN), F32)
        h_ref[...] = jnp.zeros(h_ref.shape, F32)

    x = xr_ref[...]
    xbuf_ref[pad:pad + ts, :] = x
    xc = cb_ref[...] + x * cw_ref[CONV_W - 1:CONV_W, :]
    for j in range(1, CONV_W):
        xc = xc + xbuf_ref[pad - j:pad - j + ts, :] * cw_ref[CONV_W - 1 - j:CONV_W - j, :]
    xbuf_ref[0:pad, :] = x[ts - pad:, :]

    a, u = _lru_gates(xc, wr_ref, br_ref, wi_ref, bi_ref, lam_ref)
    sub = lax.broadcasted_iota(jnp.int32, (ts, D_RNN), 0) % SUBLANES
    for d in (1, 2, 4):
        keep = sub >= d
        a_sh = jnp.where(keep, pltpu.roll(a, d, 0), 1.0)
        u_sh = jnp.where(keep, pltpu.roll(u, d, 0), 0.0)
        u = a * u_sh + u
        a = a * a_sh
    a_ref[...] = a
    u_ref[...] = u

    def body(g, h):
        sl = pl.ds(pl.multiple_of(g * SUBLANES, SUBLANES), SUBLANES)
        hh = a_ref[sl, :] * h + u_ref[sl, :]
        u_ref[sl, :] = hh
        return hh[SUBLANES - 1:SUBLANES, :]

    h_last = lax.fori_loop(0, ts // SUBLANES, body, h_ref[...])
    h_ref[...] = h_last
    pr_ref[...] = (jax.nn.gelu(yr_ref[...]) * u_ref[...]).astype(BF16)

    @pl.when(i == pl.num_programs(1) - 1)
    def _():
        hl_ref[...] = h_last


def _scan(xr, yr, cw, cb, wr, br, wi, bi, lam, ts):
    b, s, _ = xr.shape
    vec = pl.BlockSpec((1, D_RNN), lambda b_, i: (0, 0))
    return pl.pallas_call(
        functools.partial(_scan_kernel, ts=ts),
        grid=(b, s // ts),
        in_specs=[_row_spec(ts, D_RNN), _row_spec(ts, D_RNN),
                  pl.BlockSpec((CONV_W, D_RNN), lambda b_, i: (0, 0)), vec,
                  _resident((D_RNN, D_RNN)), vec, _resident((D_RNN, D_RNN)), vec, vec],
        out_specs=[_row_spec(ts, D_RNN), pl.BlockSpec((None, 1, D_RNN), lambda b_, i: (b_, 0, 0))],
        out_shape=[jax.ShapeDtypeStruct((b, s, D_RNN), BF16), jax.ShapeDtypeStruct((b, 1, D_RNN), F32)],
        scratch_shapes=[pltpu.VMEM((ts + SUBLANES, D_RNN), F32), pltpu.VMEM((ts, D_RNN), F32),
                        pltpu.VMEM((ts, D_RNN), F32), pltpu.VMEM((1, D_RNN), F32)],
        compiler_params=_params(("parallel", "arbitrary")),
        name="conv_lru_scan",
    )(xr, yr, cw, cb, wr, br, wi, bi, lam)


def _lru_step_kernel(xr_ref, yr_ref, c0_ref, c1_ref, c2_ref, h0_ref, cw_ref, cb_ref,
                     wr_ref, br_ref, wi_ref, bi_ref, lam_ref, pr_ref, h_ref):
    x = xr_ref[...]
    xc = cb_ref[...] + c0_ref[...] * cw_ref[0:1, :] + c1_ref[...] * cw_ref[1:2, :] \
        + c2_ref[...] * cw_ref[2:3, :] + x * cw_ref[3:4, :]
    a, u = _lru_gates(xc, wr_ref, br_ref, wi_ref, bi_ref, lam_ref)
    h = a * h0_ref[...] + u
    h_ref[...] = h
    pr_ref[...] = (jax.nn.gelu(yr_ref[...]) * h).astype(BF16)


def _lru_step(xr, yr, c0, c1, c2, h0, cw, cb, wr, br, wi, bi, lam):
    n = xr.shape[0]
    return pl.pallas_call(
        _lru_step_kernel,
        out_shape=[jax.ShapeDtypeStruct((n, D_RNN), BF16), jax.ShapeDtypeStruct((n, D_RNN), F32)],
        compiler_params=pltpu.CompilerParams(vmem_limit_bytes=VMEM_LIMIT),
        name="lru_step",
    )(xr, yr, c0, c1, c2, h0, cw, cb, wr, br, wi, bi, lam)


def _merge_kernel(of_ref, pr_ref, ga_ref, gr_ref, x_ref, gm_ref, wpa_ref, wpr_ref, wo_ref,
                  g_ref, b_ref, o_ref):
    p_a = jnp.dot(of_ref[...], wpa_ref[...], preferred_element_type=F32)
    p_r = jnp.dot(pr_ref[...], wpr_ref[...], preferred_element_type=F32)
    mrg = jax.nn.sigmoid(ga_ref[...]) * p_a + jax.nn.sigmoid(gr_ref[...]) * p_r
    out = jnp.dot(mrg.astype(BF16), wo_ref[...], preferred_element_type=F32)
    o_ref[...] = _layer_norm(ALPHA * x_ref[...] + gm_ref[...] * out, g_ref[...], b_ref[...])


def _merge(of, pr, ga, gr, x, gm, wpa, wpr, wo, ln_g, ln_b, tm):
    bx, sx, _ = x.shape
    vec = pl.BlockSpec((1, D_MODEL), lambda b, i: (0, 0))
    row = _row_spec(tm, D_MODEL)
    sq = _resident((D_MODEL, D_MODEL))
    return pl.pallas_call(
        _merge_kernel,
        grid=(bx, sx // tm),
        in_specs=[row, row, row, row, row, _mod_spec(gm, tm), sq, sq, sq, vec, vec],
        out_specs=row,
        out_shape=jax.ShapeDtypeStruct((bx, sx, D_MODEL), F32),
        compiler_params=_params(("parallel", "parallel")),
        name="merge_norm",
    )(of, pr, ga, gr, x, gm, wpa, wpr, wo, ln_g, ln_b)


def _ffn_kernel(x_ref, sc_ref, sh_ref, gf_ref, win_ref, wout_ref, g_ref, b_ref, o_ref):
    x = x_ref[...]
    h = (x * (1.0 + sc_ref[...]) + sh_ref[...]).astype(BF16)
    gate = jnp.dot(h, win_ref[:, :D_FF], preferred_element_type=F32)
    up = jnp.dot(h, win_ref[:, D_FF:], preferred_element_type=F32)
    act = (jax.nn.silu(gate) * up).astype(BF16)
    out = jnp.dot(act, wout_ref[...], preferred_element_type=F32)
    o_ref[...] = _layer_norm(ALPHA * x + gf_ref[...] * out, g_ref[...], b_ref[...])


def _ffn(x, sc, sh, gf, win, wout, ln_g, ln_b, tm):
    bx, sx, _ = x.shape
    vec = pl.BlockSpec((1, D_MODEL), lambda b, i: (0, 0))
    row = _row_spec(tm, D_MODEL)
    return pl.pallas_call(
        _ffn_kernel,
        grid=(bx, sx // tm),
        in_specs=[row, _mod_spec(sc, tm), _mod_spec(sh, tm), _mod_spec(gf, tm),
                  _resident((D_MODEL, 2 * D_FF)), _resident((D_FF, D_MODEL)), vec, vec],
        out_specs=row,
        out_shape=jax.ShapeDtypeStruct((bx, sx, D_MODEL), F32),
        compiler_params=_params(("parallel", "parallel")),
        name="ffn_norm",
    )(x, sc, sh, gf, win, wout, ln_g, ln_b)


def _block_diag(w):
    eye = jnp.eye(N_RNN_BLOCKS, dtype=w.dtype)
    return jnp.einsum('nde,nm->ndme', w, eye).reshape(D_RNN, D_RNN)


def _expand_query(q):
    n = q.shape[0]
    q5 = q.reshape(n, N_KV_HEADS, GROUP, 2, QK_DIM)
    eye_h = jnp.eye(N_KV_HEADS, dtype=q.dtype)
    eye_c = jnp.eye(2, dtype=q.dtype)
    qt = jnp.einsum('nhgcd,hH,cC->nchgHCd', q5, eye_h, eye_c)
    return qt.reshape(n, 2 * N_HEADS, KC)


def kernel(x_prompt, x_sample, c_prompt, c_sample, cache_k, cache_v, page_table, state_h, state_conv,
           w_ada, b_ada, w_in, lambda_q1, lambda_k1, lambda_q2, lambda_k2, subln_g, conv_w, conv_b,
           w_rgate, b_rgate, w_igate, b_igate, lru_lambda, w_pa, w_pr, w_o,
           ln1_g, ln1_b, ln2_g, ln2_b, w_ffn_in, w_ffn_out):
    bp, sp, _ = x_prompt.shape
    bs, ss, _ = x_sample.shape
    assert ss == 1
    n_pages = page_table.shape[1]
    past_len = n_pages * PAGE_SIZE
    n_pool = cache_k.shape[1]

    cache_kt = jnp.transpose(cache_k, (0, 1, 3, 4, 5, 2)).reshape(DEPTH, n_pool, KC, PAGE_SIZE)
    cache_vr = cache_v.reshape(DEPTH, n_pool, PAGE_SIZE * N_KV_HEADS, V_DIM)

    inv = ROPE_THETA ** (-jnp.arange(ROT_DIM // 2, dtype=F32) * 2.0 / ROT_DIM)
    inv_lane = jnp.tile(inv, LANES // (ROT_DIM // 2)).reshape(1, LANES)
    cos_p, sin_p = _rope_tables(inv_lane, sp, 0)
    cos_s, sin_s = _rope_tables(inv_lane, ss, past_len)

    pad = (-bp) % SUBLANES
    c_all = jnp.concatenate([c_prompt, jnp.zeros((pad, D_MODEL), F32), c_sample], axis=0)
    mod = _ada_mod(c_all, w_ada, b_ada)
    mod_p = mod[:, :bp].reshape(DEPTH, bp, 1, 6, D_MODEL)
    mod_s = mod[:, bp + pad:].reshape(DEPTH, 1, bs, 6, D_MODEL)

    w_in_b = w_in.astype(BF16)
    w_pa_b = w_pa.astype(BF16)
    w_pr_b = w_pr.astype(BF16)
    w_o_b = w_o.astype(BF16)
    w_fi_b = w_ffn_in.astype(BF16)
    w_fo_b = w_ffn_out.astype(BF16)

    yp = x_prompt
    ys = x_sample.reshape(1, bs, D_MODEL)
    outs = [[] for _ in range(8)]
    for l in range(DEPTH):
        lam_init = 0.8 - 0.6 * math.exp(-0.3 * l)
        lvecs = [a[l].reshape(1, QK_DIM) for a in (lambda_q1, lambda_k1, lambda_q2, lambda_k2)]
        sg = subln_g[l].reshape(1, V_DIM)
        wr = _block_diag(w_rgate[l]).astype(BF16)
        wi = _block_diag(w_igate[l]).astype(BF16)
        vecs = [a[l].reshape(1, -1) for a in (conv_b, b_rgate, b_igate, lru_lambda,
                                              ln1_g, ln1_b, ln2_g, ln2_b)]
        cb, br, bi, lam_lru, g1, b1, g2, b2 = vecs
        cw = conv_w[l]

        m = [mod_p[l][:, :, i, :] for i in range(6)]
        q, kf, vf, kb, vb, xr, yr, ga, gr = _inproj(yp, m[1], m[0], cos_p, sin_p, w_in_b[l], 256)
        of = _flash(q, kb, vb, lvecs, sg, lam_init, 512)
        pr, h_last = _scan(xr, yr, cw, cb, wr, br, wi, bi, lam_lru, 256)
        y1 = _merge(of, pr, ga, gr, yp, m[2], w_pa_b[l], w_pr_b[l], w_o_b[l], g1, b1, 256)
        yp = _ffn(y1, m[4], m[3], m[5], w_fi_b[l], w_fo_b[l], g2, b2, 256)
        outs[0].append(kf.reshape(bp, sp, N_KV_HEADS, 2, QK_DIM))
        outs[1].append(vf.reshape(bp, sp, N_KV_HEADS, V_DIM))
        outs[2].append(h_last.reshape(bp, D_RNN))
        outs[3].append(xr[:, sp - (CONV_W - 1):, :])

        m = [mod_s[l][:, :, i, :] for i in range(6)]
        q, kf, vf, kb, vb, xr, yr, ga, gr = _inproj(ys, m[1], m[0], cos_s, sin_s, w_in_b[l], bs)
        qt = _expand_query(q[0])
        of = _decode_attn(page_table, qt, kb.reshape(bs, 1, KC), vb.reshape(bs, 1, VC),
                          cache_kt, cache_vr, l, lvecs, sg, lam_init, 8)
        sc_l = state_conv[l]
        pr, h_new = _lru_step(xr[0], yr[0], sc_l[:, 0], sc_l[:, 1], sc_l[:, 2], state_h[l],
                              cw, cb, wr, br, wi, bi, lam_lru)
        y1 = _merge(of.reshape(1, bs, D_MODEL), pr.reshape(1, bs, D_MODEL), ga, gr, ys, m[2],
                    w_pa_b[l], w_pr_b[l], w_o_b[l], g1, b1, bs)
        ys = _ffn(y1, m[4], m[3], m[5], w_fi_b[l], w_fo_b[l], g2, b2, bs)
        outs[4].append(kf.reshape(bs, 1, N_KV_HEADS, 2, QK_DIM))
        outs[5].append(vf.reshape(bs, 1, N_KV_HEADS, V_DIM))
        outs[6].append(h_new)
        outs[7].append(jnp.concatenate([sc_l[:, 1:], xr[0][:, None, :]], axis=1))

    return (yp, ys.reshape(bs, 1, D_MODEL)) + tuple(jnp.stack(o) for o in outs)
```
